```python
import math
import jax, jax.numpy as jnp
from jax import lax
import numpy as np

D_MODEL = 2048
BATCH = 4
SEQ = 2048
DEPTH = 2

CONV_DIM = D_MODEL
CONV_WIDTH = 3
GDN_K_HEADS = 16
GDN_V_HEADS = 32
GDN_K_DIM = D_MODEL // GDN_K_HEADS
GDN_V_DIM = D_MODEL // GDN_V_HEADS
GDN_QK_WIDTH = GDN_K_HEADS * GDN_K_DIM
GDN_V_WIDTH = GDN_V_HEADS * GDN_V_DIM
GDN_CONV_WIDTH = 4
GDN_CHUNK = 64
FOX_HEADS = 16
FOX_DIM = D_MODEL // FOX_HEADS
FOX_WIDTH = FOX_HEADS * FOX_DIM
Q_BLOCK = 128
D_FF = 4 * D_MODEL
LN_EPS = 1e-5
NORM_EPS = 1e-6
DEEPNORM_ALPHA = (2 * DEPTH) ** 0.25
DEEPNORM_BETA = (8 * DEPTH) ** -0.25
SPLIT_SIZES = (CONV_DIM, CONV_DIM, CONV_DIM,
               GDN_QK_WIDTH, GDN_QK_WIDTH, GDN_V_WIDTH, GDN_V_WIDTH,
               GDN_V_HEADS, GDN_V_HEADS,
               FOX_WIDTH, FOX_WIDTH, FOX_WIDTH, FOX_HEADS,
               D_MODEL, D_MODEL, D_MODEL)
IN_WIDTH = sum(SPLIT_SIZES)

kernel_name = 'hybrid_conv_gdn_fox_gated_merge_deepnorm_adaln'


def layer_norm(x, g, b):
    xf = x.astype(jnp.float32)
    mu = jnp.mean(xf, axis=-1, keepdims=True)
    var = jnp.mean(jnp.square(xf - mu), axis=-1, keepdims=True)
    return ((xf - mu) * lax.rsqrt(var + LN_EPS) * g + b).astype(x.dtype)


def causal_depthwise_conv(x, w):
    width = w.shape[0]
    seq = x.shape[1]
    xp = jnp.pad(x, ((0, 0), (width - 1, 0), (0, 0)))
    out = xp[:, 0:seq] * w[0]
    for i in range(1, width):
        out = out + xp[:, i:i + seq] * w[i]
    return out


def l2_normalize(t):
    tf = t.astype(jnp.float32)
    return tf * lax.rsqrt(jnp.sum(tf * tf, axis=-1, keepdims=True) + NORM_EPS)


def short_conv_branch(b_gate, c_gate, h, conv_w):
    return b_gate * causal_depthwise_conv(c_gate * h, conv_w)


def chunked_gated_delta_rule(q, k, v, beta, g):
    bsz, seq, heads, dk = q.shape
    dv = v.shape[-1]
    n_chunks = seq // GDN_CHUNK

    def chunks(t):
        t = t.reshape((bsz, n_chunks, GDN_CHUNK, heads) + t.shape[3:])
        return jnp.moveaxis(t, 3, 1)

    q = chunks(q) * (dk ** -0.5)
    k = chunks(k)
    v = chunks(v)
    beta = chunks(beta)
    g = jnp.cumsum(chunks(g), axis=-1)
    tril = jnp.tril(jnp.ones((GDN_CHUNK, GDN_CHUNK), dtype=bool))
    strict = jnp.tril(jnp.ones((GDN_CHUNK, GDN_CHUNK), dtype=bool), k=-1)
    decay = jnp.exp(jnp.where(tril, g[..., :, None] - g[..., None, :], -jnp.inf))
    k_beta = k * beta[..., None]
    lower = jnp.where(strict, jnp.einsum('bhncd,bhnsd->bhncs', k_beta, k) * decay, 0.0)
    eye = jnp.eye(GDN_CHUNK, dtype=lower.dtype)
    rhs = jnp.concatenate([v * beta[..., None], k_beta * jnp.exp(g)[..., None]], axis=-1)
    sol = lax.linalg.triangular_solve(lower + eye, rhs, left_side=True, lower=True, unit_diagonal=True)
    u = sol[..., :dv]
    w = sol[..., dv:]
    intra = jnp.where(tril, jnp.einsum('bhncd,bhnsd->bhncs', q, k) * decay, 0.0)
    q_decay = q * jnp.exp(g)[..., None]
    k_state = k * jnp.exp(g[..., -1:] - g)[..., None]
    chunk_decay = jnp.exp(g[..., -1])

    def step(state, xs):
        u_i, w_i, qd_i, intra_i, ks_i, cd_i = xs
        v_new = u_i - jnp.einsum('bhck,bhkv->bhcv', w_i, state)
        o_i = jnp.einsum('bhck,bhkv->bhcv', qd_i, state) + jnp.einsum('bhcs,bhsv->bhcv', intra_i, v_new)
        state = state * cd_i[..., None, None] + jnp.einsum('bhck,bhcv->bhkv', ks_i, v_new)
        return state, o_i

    xs = tuple(jnp.moveaxis(t, 2, 0) for t in (u, w, q_decay, intra, k_state, chunk_decay))
    state0 = jnp.zeros((bsz, heads, dk, dv), jnp.float32)
    _, o = lax.scan(step, state0, xs)
    return jnp.transpose(o, (1, 0, 3, 2, 4)).reshape(bsz, seq, heads, dv)


def gated_deltanet_branch(q, k, v, z, b_logit, a_logit, conv_w, a_log, dt_bias, norm_w):
    bsz, seq, _ = q.shape
    qkv = jax.nn.silu(causal_depthwise_conv(jnp.concatenate([q, k, v], axis=-1), conv_w))
    q, k, v = jnp.split(qkv, [GDN_QK_WIDTH, 2 * GDN_QK_WIDTH], axis=-1)
    rep = GDN_V_HEADS // GDN_K_HEADS
    q = jnp.repeat(l2_normalize(q.reshape(bsz, seq, GDN_K_HEADS, GDN_K_DIM)), rep, axis=2)
    k = jnp.repeat(l2_normalize(k.reshape(bsz, seq, GDN_K_HEADS, GDN_K_DIM)), rep, axis=2)
    v = v.reshape(bsz, seq, GDN_V_HEADS, GDN_V_DIM).astype(jnp.float32)
    beta = jax.nn.sigmoid(b_logit.astype(jnp.float32))
    g = -jnp.exp(a_log.astype(jnp.float32)) * jax.nn.softplus(a_logit.astype(jnp.float32) + dt_bias.astype(jnp.float32))
    o = chunked_gated_delta_rule(q, k, v, beta, g)
    o = o * lax.rsqrt(jnp.mean(o * o, axis=-1, keepdims=True) + NORM_EPS) * norm_w
    o = o * jax.nn.silu(z.reshape(bsz, seq, GDN_V_HEADS, GDN_V_DIM).astype(jnp.float32))
    return o.reshape(bsz, seq, GDN_V_WIDTH).astype(z.dtype)


def forgetting_attention_branch(q, k, v, f_logit, f_bias):
    bsz, seq, _ = q.shape

    def heads(t):
        return t.reshape(bsz, seq, FOX_HEADS, FOX_DIM).transpose(0, 2, 1, 3)

    q, k, v = heads(q), heads(k), heads(v)
    log_f = jax.nn.log_sigmoid(f_logit.astype(jnp.float32) + f_bias.astype(jnp.float32))
    cum = jnp.cumsum(log_f, axis=1).transpose(0, 2, 1)
    scale = FOX_DIM ** -0.5
    outs = []
    for blk in range(seq // Q_BLOCK):
        q0, q1 = blk * Q_BLOCK, (blk + 1) * Q_BLOCK
        s = jnp.einsum('bhqd,bhkd->bhqk', q[:, :, q0:q1], k[:, :, :q1],
                       preferred_element_type=jnp.float32) * scale
        s = s + cum[:, :, q0:q1, None] - cum[:, :, None, :q1]
        causal = (q0 + jnp.arange(Q_BLOCK))[:, None] >= jnp.arange(q1)[None, :]
        p = jax.nn.softmax(jnp.where(causal, s, -jnp.inf), axis=-1)
        outs.append(jnp.einsum('bhqk,bhkd->bhqd', p.astype(v.dtype), v[:, :, :q1]))
    o = jnp.concatenate(outs, axis=2)
    return o.transpose(0, 2, 1, 3).reshape(bsz, seq, FOX_WIDTH)


def setup_inputs(seed: int = 0) -> dict:
    key = jax.random.key(seed)
    ks = jax.random.split(key, 24)
    f32 = jnp.float32

    def nrm(k, shape, scale):
        return jax.random.normal(k, shape, f32) * scale

    x = nrm(ks[0], (BATCH, SEQ, D_MODEL), 1.0)
    c = nrm(ks[1], (BATCH, D_MODEL), 1.0)
    w_ada = nrm(ks[2], (DEPTH, D_MODEL, 6 * D_MODEL), 0.5 * D_MODEL ** -0.5)
    b_ada = nrm(ks[3], (DEPTH, 6 * D_MODEL), 0.02)
    w_in = nrm(ks[4], (DEPTH, D_MODEL, IN_WIDTH), D_MODEL ** -0.5)
    conv_w = nrm(ks[5], (DEPTH, CONV_WIDTH, CONV_DIM), CONV_WIDTH ** -0.5)
    gdn_conv_w = nrm(ks[6], (DEPTH, GDN_CONV_WIDTH, 2 * GDN_QK_WIDTH + GDN_V_WIDTH), GDN_CONV_WIDTH ** -0.5)
    gdn_a_log = jnp.log(jax.random.uniform(ks[7], (DEPTH, GDN_V_HEADS), f32, 1.0, 16.0))
    dt = jnp.exp(jax.random.uniform(ks[8], (DEPTH, GDN_V_HEADS), f32, math.log(1e-3), math.log(1e-1)))
    gdn_dt_bias = dt + jnp.log(-jnp.expm1(-dt))
    gdn_norm_w = 1.0 + nrm(ks[9], (DEPTH, GDN_V_DIM), 0.02)
    fox_f_bias = jax.random.uniform(ks[10], (DEPTH, FOX_HEADS), f32, 1.0, 4.0)
    w_branch_a = nrm(ks[11], (DEPTH, CONV_DIM, D_MODEL), DEEPNORM_BETA * CONV_DIM ** -0.5)
    w_branch_b = nrm(ks[12], (DEPTH, GDN_V_WIDTH, D_MODEL), DEEPNORM_BETA * GDN_V_WIDTH ** -0.5)
    w_branch_c = nrm(ks[13], (DEPTH, FOX_WIDTH, D_MODEL), DEEPNORM_BETA * FOX_WIDTH ** -0.5)
    w_o = nrm(ks[14], (DEPTH, D_MODEL, D_MODEL), DEEPNORM_BETA * D_MODEL ** -0.5)
    ln1_g = 1.0 + nrm(ks[15], (DEPTH, D_MODEL), 0.02)
    ln1_b = nrm(ks[16], (DEPTH, D_MODEL), 0.02)
    w_ff1 = nrm(ks[17], (DEPTH, D_MODEL, D_FF), D_MODEL ** -0.5)
    w_ff2 = nrm(ks[18], (DEPTH, D_FF, D_MODEL), DEEPNORM_BETA * D_FF ** -0.5)
    ln2_g = 1.0 + nrm(ks[19], (DEPTH, D_MODEL), 0.02)
    ln2_b = nrm(ks[20], (DEPTH, D_MODEL), 0.02)
    return {'x': x, 'c': c, 'w_ada': w_ada, 'b_ada': b_ada, 'w_in': w_in, 'conv_w': conv_w,
            'gdn_conv_w': gdn_conv_w, 'gdn_a_log': gdn_a_log, 'gdn_dt_bias': gdn_dt_bias,
            'gdn_norm_w': gdn_norm_w, 'fox_f_bias': fox_f_bias, 'w_branch_a': w_branch_a,
            'w_branch_b': w_branch_b, 'w_branch_c': w_branch_c, 'w_o': w_o, 'ln1_g': ln1_g,
            'ln1_b': ln1_b, 'w_ff1': w_ff1, 'w_ff2': w_ff2, 'ln2_g': ln2_g, 'ln2_b': ln2_b}


def reference(x, c, w_ada, b_ada, w_in, conv_w, gdn_conv_w, gdn_a_log, gdn_dt_bias, gdn_norm_w,
              fox_f_bias, w_branch_a, w_branch_b, w_branch_c, w_o, ln1_g, ln1_b, w_ff1, w_ff2,
              ln2_g, ln2_b):
    split_points = np.cumsum(SPLIT_SIZES)[:-1].tolist()
    c_act = jax.nn.silu(c)
    for l in range(DEPTH):
        mod = c_act @ w_ada[l] + b_ada[l]
        shift1, scale1, gate1, shift2, scale2, gate2 = jnp.split(mod[:, None, :], 6, axis=-1)

        u = x * (1 + scale1) + shift1
        (cb, cc, ch, gq, gk, gv, gz, gb, ga, fq, fk, fv, ff, ma, mb, mc) = jnp.split(
            u @ w_in[l], split_points, axis=-1)
        y_a = short_conv_branch(cb, cc, ch, conv_w[l])
        y_b = gated_deltanet_branch(gq, gk, gv, gz, gb, ga, gdn_conv_w[l], gdn_a_log[l],
                                    gdn_dt_bias[l], gdn_norm_w[l])
        y_c = forgetting_attention_branch(fq, fk, fv, ff, fox_f_bias[l])
        merged = (jax.nn.sigmoid(ma) * (y_a @ w_branch_a[l])
                  + jax.nn.sigmoid(mb) * (y_b @ w_branch_b[l])
                  + jax.nn.sigmoid(mc) * (y_c @ w_branch_c[l]))
        x = layer_norm(DEEPNORM_ALPHA * x + gate1 * (merged @ w_o[l]), ln1_g[l], ln1_b[l])

        u = x * (1 + scale2) + shift2
        h = jnp.square(jax.nn.relu(u @ w_ff1[l]))
        x = layer_norm(DEEPNORM_ALPHA * x + gate2 * (h @ w_ff2[l]), ln2_g[l], ln2_b[l])
    return x
```

```python
import functools
import math

import jax
import jax.numpy as jnp
from jax import lax
from jax.experimental import pallas as pl
from jax.experimental.pallas import tpu as pltpu

D_MODEL = 2048
DEPTH = 2
CONV_WIDTH = 3
GDN_K_HEADS = 16
GDN_V_HEADS = 32
GDN_K_DIM = 128
GDN_V_DIM = 64
GDN_CONV_WIDTH = 4
GDN_CHUNK = 64
GDN_GROUPS = 4
GDN_KH_PER_GROUP = GDN_K_HEADS // GDN_GROUPS
GDN_VH_PER_GROUP = GDN_V_HEADS // GDN_GROUPS
FOX_HEADS = 16
FOX_DIM = 128
D_FF = 4 * D_MODEL
LN_EPS = 1e-5
NORM_EPS = 1e-6
DEEPNORM_ALPHA = (2 * DEPTH) ** 0.25

OFF_GDN_B = 7 * D_MODEL
OFF_GDN_A = OFF_GDN_B + GDN_V_HEADS
OFF_FOX = OFF_GDN_A + GDN_V_HEADS
OFF_FOX_F = OFF_FOX + 3 * D_MODEL
OFF_MERGE = OFF_FOX_F + FOX_HEADS

LANES = 128
SUBLANES = 8
VMEM_LIMIT = 56 * 1024 * 1024

BF16 = jnp.bfloat16
F32 = jnp.float32


def _cparams(n_axes):
    return pltpu.CompilerParams(dimension_semantics=("arbitrary",) * n_axes,
                                vmem_limit_bytes=VMEM_LIMIT)


def _dot(a, b):
    return jnp.dot(a, b, preferred_element_type=F32)


def _dot_nt(a, b):
    return lax.dot_general(a, b, (((1,), (1,)), ((), ())), preferred_element_type=F32)


def _dot_tn(a, b):
    return lax.dot_general(a, b, (((0,), (0,)), ((), ())), preferred_element_type=F32)


def _split3(x):
    hi = x.astype(BF16)
    r1 = x - hi.astype(F32)
    mid = r1.astype(BF16)
    lo = (r1 - mid.astype(F32)).astype(BF16)
    return hi, mid, lo


def _cumsum_rows(tril_bf16, x):
    hi, mid, lo = _split3(x)
    return _dot(tril_bf16, hi) + _dot(tril_bf16, mid) + _dot(tril_bf16, lo)


def _softplus(x):
    return jnp.maximum(x, 0.0) + jnp.log(1.0 + jnp.exp(-jnp.abs(x)))


def _sigmoid(x):
    return 1.0 / (1.0 + jnp.exp(-x))


def _silu(x):
    return x * _sigmoid(x)


def _shift_rows(cur, prev8, k):
    r = pltpu.roll(cur, k, 0)
    pr = pltpu.roll(prev8, k, 0)
    row = lax.broadcasted_iota(jnp.int32, pr.shape, 0)
    first = jnp.where(row < k, pr, r[:SUBLANES])
    return jnp.concatenate([first, r[SUBLANES:]], axis=0)


def _layer_norm(t, g, b):
    mu = jnp.mean(t, axis=-1, keepdims=True)
    d = t - mu
    var = jnp.mean(d * d, axis=-1, keepdims=True)
    return d * lax.rsqrt(var + LN_EPS) * g + b


def _mod_kernel(c_ref, w_ref, b_ref, o_ref):
    c = c_ref[...]
    ca = _silu(c).astype(BF16)
    o_ref[...] = _dot(ca, w_ref[...].astype(BF16)) + b_ref[...]


def _adaln_mod(c, w_ada, b_ada):
    bsz = c.shape[0]
    rows = SUBLANES * pl.cdiv(bsz, SUBLANES)
    c_pad = jnp.zeros((rows, D_MODEL), F32).at[:bsz].set(c)
    n = w_ada.shape[-1]
    tn = 1024
    out = pl.pallas_call(
        _mod_kernel,
        grid=(DEPTH, n // tn),
        in_specs=[pl.BlockSpec((rows, D_MODEL), lambda l, j: (0, 0)),
                  pl.BlockSpec((None, D_MODEL, tn), lambda l, j: (l, 0, j)),
                  pl.BlockSpec((None, 1, tn), lambda l, j: (l, 0, j))],
        out_specs=pl.BlockSpec((None, rows, tn), lambda l, j: (l, 0, j)),
        out_shape=jax.ShapeDtypeStruct((DEPTH, rows, n), F32),
        compiler_params=_cparams(2),
    )(c_pad, w_ada, b_ada.reshape(DEPTH, 1, n))
    return out[:, :bsz]


def _modulate_kernel(x_ref, sc_ref, sh_ref, o_ref):
    o_ref[...] = (x_ref[...] * (1.0 + sc_ref[...]) + sh_ref[...]).astype(o_ref.dtype)


def _modulate(x, scale, shift):
    bsz, seq, d = x.shape
    ts = min(seq, 512)
    row = pl.BlockSpec((None, 1, d), lambda b, s: (b, 0, 0))
    return pl.pallas_call(
        _modulate_kernel,
        grid=(bsz, seq // ts),
        in_specs=[pl.BlockSpec((None, ts, d), lambda b, s: (b, s, 0)), row, row],
        out_specs=pl.BlockSpec((None, ts, d), lambda b, s: (b, s, 0)),
        out_shape=jax.ShapeDtypeStruct((bsz, seq, d), BF16),
        compiler_params=_cparams(2),
    )(x, scale, shift)


def _matmul_ws_kernel(a_ref, w_ref, o_ref, wbf_ref, *, relu_sq):
    @pl.when(pl.program_id(1) == 0)
    def _():
        wbf_ref[...] = w_ref[...].astype(BF16)

    acc = _dot(a_ref[...], wbf_ref[...])
    if relu_sq:
        acc = jnp.square(jnp.maximum(acc, 0.0))
    o_ref[...] = acc.astype(o_ref.dtype)


def _matmul_ws(a, w, layer, n_cols, *, tm, tn, out_dtype, relu_sq=False):
    m, k = a.shape
    assert m % tm == 0 and n_cols % tn == 0
    return pl.pallas_call(
        functools.partial(_matmul_ws_kernel, relu_sq=relu_sq),
        grid=(n_cols // tn, m // tm),
        in_specs=[pl.BlockSpec((tm, k), lambda j, i: (i, 0)),
                  pl.BlockSpec((None, k, tn), lambda j, i: (layer, 0, j))],
        out_specs=pl.BlockSpec((tm, tn), lambda j, i: (i, j)),
        out_shape=jax.ShapeDtypeStruct((m, n_cols), out_dtype),
        scratch_shapes=[pltpu.VMEM((k, tn), BF16)],
        compiler_params=_cparams(2),
    )(a, w)


def _conv_a_kernel(cb_ref, cc_ref, ch_ref, ccp_ref, chp_ref, w_ref, o_ref):
    g = cc_ref[...].astype(F32) * ch_ref[...].astype(F32)
    gp = ccp_ref[...].astype(F32) * chp_ref[...].astype(F32)
    gp = jnp.where(pl.program_id(1) == 0, 0.0, gp)
    w = w_ref[...]
    acc = g * w[CONV_WIDTH - 1:CONV_WIDTH]
    for k in range(1, CONV_WIDTH):
        acc = acc + _shift_rows(g, gp, k) * w[CONV_WIDTH - 1 - k:CONV_WIDTH - k]
    o_ref[...] = (cb_ref[...].astype(F32) * acc).astype(o_ref.dtype)


def _prev8_spec(ts, w, col):
    return pl.BlockSpec((None, SUBLANES, w),
                        lambda b, s: (b, jnp.maximum(s * (ts // SUBLANES) - 1, 0), col))


def _conv_branch_a(p1, conv_w_l):
    bsz, seq, _ = p1.shape
    ts = min(seq, 256)
    w = D_MODEL

    def cur(col):
        return pl.BlockSpec((None, ts, w), lambda b, s: (b, s, col))

    return pl.pallas_call(
        _conv_a_kernel,
        grid=(bsz, seq // ts),
        in_specs=[cur(0), cur(1), cur(2), _prev8_spec(ts, w, 1), _prev8_spec(ts, w, 2),
                  pl.BlockSpec((CONV_WIDTH, w), lambda b, s: (0, 0))],
        out_specs=pl.BlockSpec((None, ts, w), lambda b, s: (b, s, 0)),
        out_shape=jax.ShapeDtypeStruct((bsz, seq, w), BF16),
        compiler_params=_cparams(2),
    )(p1, p1, p1, p1, p1, conv_w_l)


def _gdn_pre_kernel(x_ref, xp_ref, w_ref, o_ref, *, normalize):
    x = x_ref[...].astype(F32)
    xp = jnp.where(pl.program_id(2) == 0, 0.0, xp_ref[...].astype(F32))
    w = w_ref[...]
    acc = x * w[GDN_CONV_WIDTH - 1:GDN_CONV_WIDTH]
    for k in range(1, GDN_CONV_WIDTH):
        acc = acc + _shift_rows(x, xp, k) * w[GDN_CONV_WIDTH - 1 - k:GDN_CONV_WIDTH - k]
    y = _silu(acc)
    if normalize:
        for h in range(y.shape[1] // GDN_K_DIM):
            yh = y[:, h * GDN_K_DIM:(h + 1) * GDN_K_DIM]
            ss = jnp.sum(yh * yh, axis=-1, keepdims=True)
            o_ref[:, h * GDN_K_DIM:(h + 1) * GDN_K_DIM] = (yh * lax.rsqrt(ss + NORM_EPS)).astype(o_ref.dtype)
    else:
        o_ref[...] = y.astype(o_ref.dtype)


def _gdn_pre(p1, gdn_conv_w_l, part):
    bsz, seq, _ = p1.shape
    ts = min(seq, 256)
    tw = 1024
    nw = D_MODEL // tw
    col0 = (3 + part) * nw
    return pl.pallas_call(
        functools.partial(_gdn_pre_kernel, normalize=part < 2),
        grid=(bsz, nw, seq // ts),
        in_specs=[pl.BlockSpec((None, ts, tw), lambda b, j, s: (b, s, col0 + j)),
                  pl.BlockSpec((None, SUBLANES, tw),
                               lambda b, j, s: (b, jnp.maximum(s * (ts // SUBLANES) - 1, 0), col0 + j)),
                  pl.BlockSpec((GDN_CONV_WIDTH, tw), lambda b, j, s: (0, part * nw + j))],
        out_specs=pl.BlockSpec((None, ts, tw), lambda b, j, s: (b, s, j)),
        out_shape=jax.ShapeDtypeStruct((bsz, seq, D_MODEL), BF16),
        compiler_params=_cparams(3),
    )(p1, p1, gdn_conv_w_l)


def _gdn_kernel(q_ref, k_ref, v_ref, z_ref, gate_ref, alog_ref, dtb_ref, nw_ref, o_ref, state_ref):
    c = GDN_CHUNK
    nh = GDN_VH_PER_GROUP

    @pl.when(pl.program_id(2) == 0)
    def _():
        state_ref[...] = jnp.zeros_like(state_ref)

    row = lax.broadcasted_iota(jnp.int32, (c, c), 0)
    col = lax.broadcasted_iota(jnp.int32, (c, c), 1)
    tril = row >= col
    strict = row > col
    tril_bf = jnp.where(tril, 1.0, 0.0).astype(BF16)
    eye = jnp.where(row == col, 1.0, 0.0).astype(F32)

    gt = gate_ref[...]
    beta_all = _sigmoid(gt)
    gl = -jnp.exp(alog_ref[...]) * _softplus(gt + dtb_ref[...])
    gc = _cumsum_rows(tril_bf, gl)
    eg = jnp.exp(gc)
    g_last = gc[c - 1:c, :]
    elm = jnp.exp(g_last - gc)
    cd = jnp.exp(g_last)
    gc_t = jnp.concatenate([gc, jnp.zeros((LANES - c, LANES), F32)], axis=0).T

    scale = GDN_K_DIM ** -0.5
    nwt = nw_ref[...]
    for jh in range(GDN_KH_PER_GROUP):
        kb = k_ref[:, jh * GDN_K_DIM:(jh + 1) * GDN_K_DIM]
        qb = q_ref[:, jh * GDN_K_DIM:(jh + 1) * GDN_K_DIM]
        kf = kb.astype(F32)
        qf = qb.astype(F32)
        gram = _dot_nt(jnp.concatenate([kb, qb], axis=0), kb)
        kk = gram[:c]
        qk = gram[c:]
        outs = []
        for e in range(2):
            lh = 2 * jh + e
            bcol = beta_all[:, lh:lh + 1]
            gcol = gc[:, nh + lh:nh + lh + 1]
            egcol = eg[:, nh + lh:nh + lh + 1]
            elmcol = elm[:, nh + lh:nh + lh + 1]
            cdh = cd[:, nh + lh:nh + lh + 1]
            grow = gc_t[nh + lh:nh + lh + 1, :c]
            decay = jnp.where(tril, jnp.exp(jnp.minimum(gcol - grow, 0.0)), 0.0)
            lower = jnp.where(strict, bcol * kk * decay, 0.0)
            p = -lower
            t = eye + p
            for _ in range(int(math.log2(c)) - 1):
                pb = p.astype(BF16)
                p = _dot(pb, pb)
                t = t + _dot(t.astype(BF16), p.astype(BF16))
            tb = t.astype(BF16)
            vh = v_ref[:, lh * GDN_V_DIM:(lh + 1) * GDN_V_DIM].astype(F32)
            u = _dot(tb, (vh * bcol).astype(BF16))
            w = _dot(tb, (kf * (bcol * egcol)).astype(BF16))
            intra = jnp.where(tril, qk * decay, 0.0) * scale
            qd = (qf * (egcol * scale)).astype(BF16)
            ks = (kf * elmcol).astype(BF16)
            st = state_ref[lh]
            stb = st.astype(BF16)
            v_new = u - _dot(w.astype(BF16), stb)
            vnb = v_new.astype(BF16)
            o = _dot(qd, stb) + _dot(intra.astype(BF16), vnb)
            state_ref[lh] = st * cdh + _dot_tn(ks, vnb)
            zh = z_ref[:, lh * GDN_V_DIM:(lh + 1) * GDN_V_DIM].astype(F32)
            on = o * lax.rsqrt(jnp.mean(o * o, axis=-1, keepdims=True) + NORM_EPS) * nwt
            outs.append(on * _silu(zh))
        o_ref[:, jh * GDN_K_DIM:(jh + 1) * GDN_K_DIM] = jnp.concatenate(outs, axis=1).astype(o_ref.dtype)


def _gdn(qn, kn, vv, p1, p3, alog_rows, dtb_rows, norm_w_l):
    bsz, seq, _ = qn.shape
    c = GDN_CHUNK
    gw = D_MODEL // GDN_GROUPS
    zcol0 = 6 * GDN_GROUPS

    def blk(col0):
        return pl.BlockSpec((None, c, gw), lambda b, g, n: (b, n, col0 + g))

    grp_row = pl.BlockSpec((None, 1, LANES), lambda b, g, n: (g, 0, 0))
    return pl.pallas_call(
        _gdn_kernel,
        grid=(bsz, GDN_GROUPS, seq // c),
        in_specs=[blk(0), blk(0), blk(0), blk(zcol0),
                  pl.BlockSpec((None, c, LANES), lambda b, g, n: (b, n, g)),
                  grp_row, grp_row,
                  pl.BlockSpec((1, GDN_V_DIM), lambda b, g, n: (0, 0))],
        out_specs=blk(0),
        out_shape=jax.ShapeDtypeStruct((bsz, seq, D_MODEL), BF16),
        scratch_shapes=[pltpu.VMEM((GDN_VH_PER_GROUP, GDN_K_DIM, GDN_V_DIM), F32)],
        compiler_params=_cparams(3),
    )(qn, kn, vv, p1, p3, alog_rows, dtb_rows, norm_w_l.reshape(1, GDN_V_DIM))


def _fox_gates_kernel(f_ref, bias_ref, cum_ref, cum_t_ref, carry_ref):
    @pl.when(pl.program_id(1) == 0)
    def _():
        carry_ref[...] = jnp.zeros_like(carry_ref)

    ts = f_ref.shape[0]
    x = f_ref[...] + bias_ref[...]
    logf = jnp.minimum(x, 0.0) - jnp.log(1.0 + jnp.exp(-jnp.abs(x)))
    row = lax.broadcasted_iota(jnp.int32, (ts, ts), 0)
    col = lax.broadcasted_iota(jnp.int32, (ts, ts), 1)
    tril_bf = jnp.where(row >= col, 1.0, 0.0).astype(BF16)
    cum = _cumsum_rows(tril_bf, logf) + carry_ref[...]
    carry_ref[...] = cum[ts - 1:ts, :]
    cum_ref[...] = cum
    cum_t_ref[...] = cum.T


def _fox_gates(p3, f_bias_row):
    bsz, seq, _ = p3.shape
    ts = min(seq, 256)
    fcol = GDN_GROUPS
    return pl.pallas_call(
        _fox_gates_kernel,
        grid=(bsz, seq // ts),
        in_specs=[pl.BlockSpec((None, ts, LANES), lambda b, s: (b, s, fcol)),
                  pl.BlockSpec((1, LANES), lambda b, s: (0, 0))],
        out_specs=[pl.BlockSpec((None, ts, LANES), lambda b, s: (b, s, 0)),
                   pl.BlockSpec((None, LANES, ts), lambda b, s: (b, 0, s))],
        out_shape=[jax.ShapeDtypeStruct((bsz, seq, LANES), F32),
                   jax.ShapeDtypeStruct((bsz, LANES, seq), F32)],
        scratch_shapes=[pltpu.VMEM((1, LANES), F32)],
        compiler_params=_cparams(2),
    )(p3, f_bias_row)


def _fox_kernel(q_ref, k_ref, v_ref, cum_ref, cum_t_ref, o_ref, *, tq):
    h = pl.program_id(1)
    qi = pl.program_id(2)
    q = q_ref[...]
    lane = lax.broadcasted_iota(jnp.int32, (tq, LANES), 1)
    cq = jnp.sum(jnp.where(lane == h, cum_ref[...], 0.0), axis=-1, keepdims=True)
    scale = FOX_DIM ** -0.5

    def scores(kt):
        k0 = pl.multiple_of(kt * tq, tq)
        s = _dot_nt(q, k_ref[pl.ds(k0, tq), :]) * scale
        return s + cq - cum_t_ref[pl.ds(h, 1), pl.ds(k0, tq)], k0

    def update(carry, s, k0):
        m, l, acc = carry
        m_new = jnp.maximum(m, jnp.max(s, axis=-1, keepdims=True))
        alpha = jnp.exp(m - m_new)
        p = jnp.exp(s - m_new)
        l = alpha * l + jnp.sum(p, axis=-1, keepdims=True)
        acc = alpha * acc + _dot(p.astype(BF16), v_ref[pl.ds(k0, tq), :])
        return m_new, l, acc

    def body(kt, carry):
        s, k0 = scores(kt)
        return update(carry, s, k0)

    init = (jnp.full((tq, 1), -1e30, F32), jnp.zeros((tq, 1), F32), jnp.zeros((tq, FOX_DIM), F32))
    carry = lax.fori_loop(0, qi, body, init)
    s, k0 = scores(qi)
    r = lax.broadcasted_iota(jnp.int32, (tq, tq), 0)
    cc = lax.broadcasted_iota(jnp.int32, (tq, tq), 1)
    s = jnp.where(r >= cc, s, -1e30)
    m, l, acc = update(carry, s, k0)
    o_ref[...] = (acc / l).astype(o_ref.dtype)


def _fox(p2, cum, cum_t):
    bsz, seq, _ = p2.shape
    tq = min(seq, 256)
    return pl.pallas_call(
        functools.partial(_fox_kernel, tq=tq),
        grid=(bsz, FOX_HEADS, seq // tq),
        in_specs=[pl.BlockSpec((None, tq, FOX_DIM), lambda b, h, i: (b, i, h)),
                  pl.BlockSpec((None, seq, FOX_DIM), lambda b, h, i: (b, 0, FOX_HEADS + h)),
                  pl.BlockSpec((None, seq, FOX_DIM), lambda b, h, i: (b, 0, 2 * FOX_HEADS + h)),
                  pl.BlockSpec((None, tq, LANES), lambda b, h, i: (b, i, 0)),
                  pl.BlockSpec((None, LANES, seq), lambda b, h, i: (b, 0, 0))],
        out_specs=pl.BlockSpec((None, tq, FOX_DIM), lambda b, h, i: (b, i, h)),
        out_shape=jax.ShapeDtypeStruct((bsz, seq, D_MODEL), BF16),
        compiler_params=_cparams(3),
    )(p2, p2, p2, cum, cum_t)


def _merge_kernel(ya_ref, yb_ref, yc_ref, ga_ref, gb_ref, gc_ref, wa_ref, wb_ref, wc_ref, o_ref,
                  wa_s, wb_s, wc_s):
    @pl.when(pl.program_id(1) == 0)
    def _():
        wa_s[...] = wa_ref[...].astype(BF16)
        wb_s[...] = wb_ref[...].astype(BF16)
        wc_s[...] = wc_ref[...].astype(BF16)

    acc = _sigmoid(ga_ref[...].astype(F32)) * _dot(ya_ref[...], wa_s[...])
    acc = acc + _sigmoid(gb_ref[...].astype(F32)) * _dot(yb_ref[...], wb_s[...])
    acc = acc + _sigmoid(gc_ref[...].astype(F32)) * _dot(yc_ref[...], wc_s[...])
    o_ref[...] = acc.astype(o_ref.dtype)


def _merge(ya, yb, yc, p2, w_a, w_b, w_c, layer):
    m, k = ya.shape
    n = D_MODEL
    tm, tn = min(m, 256), 512
    gcol0 = 3 * D_MODEL // tn

    def act():
        return pl.BlockSpec((tm, k), lambda j, i: (i, 0))

    def gate(idx):
        return pl.BlockSpec((tm, tn), lambda j, i: (i, gcol0 + idx * (n // tn) + j))

    def wt():
        return pl.BlockSpec((None, k, tn), lambda j, i: (layer, 0, j))

    return pl.pallas_call(
        _merge_kernel,
        grid=(n // tn, m // tm),
        in_specs=[act(), act(), act(), gate(0), gate(1), gate(2), wt(), wt(), wt()],
        out_specs=pl.BlockSpec((tm, tn), lambda j, i: (i, j)),
        out_shape=jax.ShapeDtypeStruct((m, n), BF16),
        scratch_shapes=[pltpu.VMEM((k, tn), BF16)] * 3,
        compiler_params=_cparams(2),
    )(ya, yb, yc, p2, p2, p2, w_a, w_b, w_c)


def _res_ln_kernel(a_ref, w_ref, x_ref, gate_ref, g_ref, b_ref, sc_ref, sh_ref, xo_ref, uo_ref, acc_ref):
    kk = pl.program_id(2)

    @pl.when(kk == 0)
    def _():
        acc_ref[...] = jnp.zeros_like(acc_ref)

    acc_ref[...] += _dot(a_ref[...], w_ref[...].astype(BF16))

    @pl.when(kk == pl.num_programs(2) - 1)
    def _():
        t = DEEPNORM_ALPHA * x_ref[...] + gate_ref[...] * acc_ref[...]
        xn = _layer_norm(t, g_ref[...], b_ref[...])
        xo_ref[...] = xn
        uo_ref[...] = (xn * (1.0 + sc_ref[...]) + sh_ref[...]).astype(uo_ref.dtype)


def _res_ln(a, w, layer, x, gate, ln_g, ln_b, nscale, nshift, *, tm, tk):
    bsz, seq, d = x.shape
    kdim = a.shape[-1]
    tm = min(tm, seq)
    a3 = a.reshape(bsz, seq, kdim)
    row = pl.BlockSpec((None, 1, d), lambda b, i, kk: (b, 0, 0))
    vec = pl.BlockSpec((1, d), lambda b, i, kk: (0, 0))
    tile = pl.BlockSpec((None, tm, d), lambda b, i, kk: (b, i, 0))
    return pl.pallas_call(
        _res_ln_kernel,
        grid=(bsz, seq // tm, kdim // tk),
        in_specs=[pl.BlockSpec((None, tm, tk), lambda b, i, kk: (b, i, kk)),
                  pl.BlockSpec((None, tk, d), lambda b, i, kk: (layer, kk, 0)),
                  tile, row, vec, vec, row, row],
        out_specs=[tile, tile],
        out_shape=[jax.ShapeDtypeStruct((bsz, seq, d), F32), jax.ShapeDtypeStruct((bsz, seq, d), BF16)],
        scratch_shapes=[pltpu.VMEM((tm, d), F32)],
        compiler_params=_cparams(3),
    )(a3, w, x, gate, ln_g.reshape(1, d), ln_b.reshape(1, d), nscale, nshift)


def _small_in_weights(w_in):
    nh = GDN_VH_PER_GROUP
    pad_g = jnp.zeros(w_in.shape[:2] + (LANES - 2 * nh,), w_in.dtype)
    parts = []
    for g in range(GDN_GROUPS):
        parts += [w_in[:, :, OFF_GDN_B + g * nh:OFF_GDN_B + (g + 1) * nh],
                  w_in[:, :, OFF_GDN_A + g * nh:OFF_GDN_A + (g + 1) * nh], pad_g]
    parts += [w_in[:, :, OFF_FOX_F:OFF_FOX_F + FOX_HEADS],
              jnp.zeros(w_in.shape[:2] + (LANES - FOX_HEADS,), w_in.dtype)]
    return jnp.concatenate(parts, axis=-1)


def _group_rows(v):
    nh = GDN_VH_PER_GROUP
    vg = v.reshape(DEPTH, GDN_GROUPS, 1, nh).astype(F32)
    return jnp.concatenate([jnp.zeros_like(vg), vg,
                            jnp.zeros((DEPTH, GDN_GROUPS, 1, LANES - 2 * nh), F32)], axis=-1)


def kernel(x, c, w_ada, b_ada, w_in, conv_w, gdn_conv_w, gdn_a_log, gdn_dt_bias, gdn_norm_w, fox_f_bias,
           w_branch_a, w_branch_b, w_branch_c, w_o, ln1_g, ln1_b, w_ff1, w_ff2, ln2_g, ln2_b):
    bsz, seq, d = x.shape
    m = bsz * seq

    mod = _adaln_mod(c, w_ada, b_ada)
    mods = [[mod[l, :, i * d:(i + 1) * d].reshape(bsz, 1, d) for i in range(6)] for l in range(DEPTH)]

    w_fox_merge = jnp.concatenate([w_in[:, :, OFF_FOX:OFF_FOX + 3 * d],
                                   w_in[:, :, OFF_MERGE:OFF_MERGE + 3 * d]], axis=-1).astype(BF16)
    w_small = _small_in_weights(w_in).astype(BF16)
    alog_rows = _group_rows(gdn_a_log)
    dtb_rows = _group_rows(gdn_dt_bias)
    f_bias_rows = jnp.concatenate([fox_f_bias.astype(F32),
                                   jnp.zeros((DEPTH, LANES - FOX_HEADS), F32)], axis=-1)

    u = _modulate(x, mods[0][1], mods[0][0])
    for l in range(DEPTH):
        shift1, scale1, gate1, shift2, scale2, gate2 = mods[l]
        u2d = u.reshape(m, d)
        p1 = _matmul_ws(u2d, w_in, l, OFF_GDN_B, tm=512, tn=1024, out_dtype=BF16).reshape(bsz, seq, OFF_GDN_B)
        p2 = _matmul_ws(u2d, w_fox_merge, l, 6 * d, tm=512, tn=1024, out_dtype=BF16)
        p3 = _matmul_ws(u2d, w_small, l, w_small.shape[-1], tm=512, tn=w_small.shape[-1],
                        out_dtype=F32).reshape(bsz, seq, -1)

        y_a = _conv_branch_a(p1, conv_w[l])
        qn = _gdn_pre(p1, gdn_conv_w[l], 0)
        kn = _gdn_pre(p1, gdn_conv_w[l], 1)
        vv = _gdn_pre(p1, gdn_conv_w[l], 2)
        y_b = _gdn(qn, kn, vv, p1, p3, alog_rows[l], dtb_rows[l], gdn_norm_w[l])
        cum, cum_t = _fox_gates(p3, f_bias_rows[l:l + 1])
        y_c = _fox(p2.reshape(bsz, seq, 6 * d), cum, cum_t)

        merged = _merge(y_a.reshape(m, d), y_b.reshape(m, d), y_c.reshape(m, d), p2,
                        w_branch_a, w_branch_b, w_branch_c, l)
        x, u = _res_ln(merged, w_o, l, x, gate1, ln1_g[l], ln1_b[l], scale2, shift2, tm=512, tk=1024)

        hid = _matmul_ws(u.reshape(m, d), w_ff1, l, D_FF, tm=512, tn=1024, out_dtype=BF16, relu_sq=True)
        if l + 1 < DEPTH:
            nscale, nshift = mods[l + 1][1], mods[l + 1][0]
        else:
            nscale, nshift = jnp.zeros_like(scale2), jnp.zeros_like(shift2)
        x, u = _res_ln(hid, w_ff2, l, x, gate2, ln2_g[l], ln2_b[l], nscale, nshift, tm=512, tk=1024)
    return x
```
